```python
import math
import jax
import jax.numpy as jnp
from jax import lax
import numpy as np

D_MODEL = 1024
BATCH = 8
SEQ = 4096
DEPTH = 4

N_A = DEPTH // 2
N_B = DEPTH - N_A
HA = 4
DK_A = D_MODEL // (2 * HA)
DV_A = D_MODEL // HA
CHUNK = 64
GATE_SOFTCAP = 15.0
A_PROJ = 2 * HA * DK_A + 2 * HA * DV_A + 2 * HA
HB = 8
HD_B = D_MODEL // (2 * HB)
QK_B = HB * 2 * HD_B
V_B = HB * 2 * HD_B
Q_BLOCK = 128
DFF = 256 * ((8 * D_MODEL // 3 + 255) // 256)
CONV_W = 3
RMS_EPS = 1e-6

kernel_name = 'yoco_mlstm_diffattn_convffn'


def rms_norm(x, g):
    xf = x.astype(jnp.float32)
    y = xf * lax.rsqrt(jnp.mean(xf * xf, axis=-1, keepdims=True) + RMS_EPS)
    return (y * g.astype(jnp.float32)).astype(x.dtype)


def mlstm_chunkwise(q, k, v, i_pre, logf):
    b_, h_, s_, dk = q.shape
    dv = v.shape[-1]
    nc = s_ // CHUNK

    def chunks(t):
        return jnp.moveaxis(t.reshape(b_, h_, nc, CHUNK, *t.shape[3:]), 2, 0)

    tri = jnp.tril(jnp.ones((CHUNK, CHUNK), dtype=bool))

    def step(carry, inp):
        c_st, n_st, m_st = carry
        qc, kc, vc, ic, fc = inp
        bcum = jnp.cumsum(fc, axis=-1)
        log_intra = jnp.where(tri, bcum[..., :, None] - bcum[..., None, :] + ic[..., None, :], -jnp.inf)
        log_inter = bcum + m_st[..., None]
        m_t = jnp.maximum(log_inter, jnp.max(log_intra, axis=-1))
        w_intra = jnp.exp(log_intra - m_t[..., None])
        w_inter = jnp.exp(log_inter - m_t)
        sc = jnp.einsum('bhtd,bhsd->bhts', qc, kc) * w_intra
        num = jnp.einsum('bhts,bhsv->bhtv', sc, vc) + w_inter[..., None] * jnp.einsum('bhvd,bhtd->bhtv', c_st, qc)
        den = jnp.sum(sc, axis=-1) + w_inter * jnp.einsum('bhd,bhtd->bht', n_st, qc)
        h_out = num / jnp.maximum(jnp.abs(den), jnp.exp(-m_t))[..., None]
        b_last = bcum[..., -1]
        log_src = b_last[..., None] - bcum + ic
        m_new = jnp.maximum(b_last + m_st, jnp.max(log_src, axis=-1))
        w_src = jnp.exp(log_src - m_new[..., None])
        decay = jnp.exp(b_last + m_st - m_new)
        c_new = decay[..., None, None] * c_st + jnp.einsum('bhs,bhsv,bhsd->bhvd', w_src, vc, kc)
        n_new = decay[..., None] * n_st + jnp.einsum('bhs,bhsd->bhd', w_src, kc)
        return (c_new, n_new, m_new), h_out

    init = (jnp.zeros((b_, h_, dv, dk), jnp.float32), jnp.zeros((b_, h_, dk), jnp.float32), jnp.zeros((b_, h_), jnp.float32))
    _, hs = lax.scan(step, init, (chunks(q), chunks(k), chunks(v), chunks(i_pre), chunks(logf)))
    return jnp.moveaxis(hs, 0, 2).reshape(b_, h_, s_, dv)


def mlstm_mixer(h, w_in, b_gate, head_g, w_out):
    b_, s_, _ = h.shape
    proj = h @ w_in
    cuts = [HA * DK_A, 2 * HA * DK_A, 2 * HA * DK_A + HA * DV_A, 2 * HA * DK_A + 2 * HA * DV_A]
    q, k, v, o, gates = jnp.split(proj, cuts, axis=-1)

    def heads(t, dh):
        return t.reshape(b_, s_, HA, dh).transpose(0, 2, 1, 3).astype(jnp.float32)

    q = heads(q, DK_A)
    k = heads(k, DK_A) * (DK_A ** -0.5)
    v = heads(v, DV_A)
    gates = gates.astype(jnp.float32) + b_gate.astype(jnp.float32)
    gates = GATE_SOFTCAP * jnp.tanh(gates / GATE_SOFTCAP)
    i_pre = gates[..., :HA].transpose(0, 2, 1)
    logf = jax.nn.log_sigmoid(gates[..., HA:]).transpose(0, 2, 1)
    hs = mlstm_chunkwise(q, k, v, i_pre, logf).transpose(0, 2, 1, 3)
    hs = rms_norm(hs, head_g)
    y = hs.reshape(b_, s_, HA * DV_A).astype(h.dtype) * jax.nn.sigmoid(o)
    return y @ w_out


def shared_kv(x, kv_norm_g, w_kv):
    b_, s_, _ = x.shape
    kv = rms_norm(x, kv_norm_g) @ w_kv
    k = kv[..., :QK_B].reshape(b_, s_, HB, 2, HD_B)
    v = kv[..., QK_B:].reshape(b_, s_, HB, 2 * HD_B)
    return k, v


def diff_attention(q, k, v, lam):
    b_, s_ = q.shape[:2]
    nq = s_ // Q_BLOCK
    qb = q.reshape(b_, nq, Q_BLOCK, HB, 2, HD_B).swapaxes(0, 1)
    starts = jnp.arange(nq) * Q_BLOCK
    kpos = jnp.arange(s_)
    vf = v.astype(jnp.float32)

    def one(args):
        qblk, st = args
        sc = jnp.einsum('bqhcd,bkhcd->bhcqk', qblk, k, preferred_element_type=jnp.float32)
        mask = (st + jnp.arange(Q_BLOCK))[:, None] >= kpos[None, :]
        p = jax.nn.softmax(jnp.where(mask, sc, -jnp.inf), axis=-1)
        a = p[:, :, 0] - lam * p[:, :, 1]
        return jnp.einsum('bhqk,bkhe->bqhe', a, vf)

    out = lax.map(one, (qb, starts))
    return out.swapaxes(0, 1).reshape(b_, s_, HB, 2 * HD_B)


def diff_mixer(h, k_sh, v_sh, w_q, lq1, lk1, lq2, lk2, subln_g, w_out, layer_idx):
    b_, s_, _ = h.shape
    lam_init = 0.8 - 0.6 * math.exp(-0.3 * layer_idx)
    lam = (jnp.exp(jnp.sum(lq1.astype(jnp.float32) * lk1.astype(jnp.float32)))
           - jnp.exp(jnp.sum(lq2.astype(jnp.float32) * lk2.astype(jnp.float32))) + lam_init)
    q = (h @ w_q).reshape(b_, s_, HB, 2, HD_B) * (HD_B ** -0.5)
    o = diff_attention(q, k_sh, v_sh, lam)
    o = rms_norm(o, subln_g) * (1.0 - lam_init)
    return o.reshape(b_, s_, V_B).astype(h.dtype) @ w_out


def conv_ffn(h, w_up, conv_w, conv_b, w_down):
    s_ = h.shape[1]
    up = h @ w_up
    g, u = up[..., :DFF], up[..., DFF:]
    gp = jnp.pad(g, ((0, 0), (CONV_W - 1, 0), (0, 0)))
    gc = conv_b + sum(conv_w[j] * gp[:, j:j + s_, :] for j in range(CONV_W))
    return (jax.nn.silu(gc) * u) @ w_down


def setup_inputs(seed: int = 0) -> dict:
    key = jax.random.key(seed)
    ks = jax.random.split(key, 24)

    def nrm(k, shape, scale):
        return jax.random.normal(k, shape, jnp.float32) * scale

    def gain(k, shape):
        return 1.0 + nrm(k, shape, 0.02)

    x = nrm(ks[0], (BATCH, SEQ, D_MODEL), 1.0)
    b_gate = jnp.concatenate([nrm(ks[5], (N_A, HA), 0.1), 3.0 + nrm(ks[6], (N_A, HA), 0.1)], axis=-1)
    return {
        'x': x,
        'attn_norm_g': gain(ks[1], (DEPTH, D_MODEL)),
        'ffn_norm_g': gain(ks[2], (DEPTH, D_MODEL)),
        'mlstm_w_in': nrm(ks[3], (N_A, D_MODEL, A_PROJ), D_MODEL ** -0.5),
        'mlstm_b_gate': b_gate,
        'mlstm_head_g': gain(ks[7], (N_A, HA, DV_A)),
        'mlstm_w_out': nrm(ks[8], (N_A, HA * DV_A, D_MODEL), (HA * DV_A) ** -0.5),
        'kv_norm_g': gain(ks[9], (D_MODEL,)),
        'w_kv': nrm(ks[10], (D_MODEL, QK_B + V_B), D_MODEL ** -0.5),
        'diff_w_q': nrm(ks[11], (N_B, D_MODEL, QK_B), D_MODEL ** -0.5),
        'diff_lambda_q1': nrm(ks[12], (N_B, HD_B), 0.1),
        'diff_lambda_k1': nrm(ks[13], (N_B, HD_B), 0.1),
        'diff_lambda_q2': nrm(ks[14], (N_B, HD_B), 0.1),
        'diff_lambda_k2': nrm(ks[15], (N_B, HD_B), 0.1),
        'diff_subln_g': gain(ks[16], (N_B, 2 * HD_B)),
        'diff_w_out': nrm(ks[17], (N_B, V_B, D_MODEL), V_B ** -0.5),
        'ffn_w_up': nrm(ks[18], (DEPTH, D_MODEL, 2 * DFF), D_MODEL ** -0.5),
        'ffn_conv_w': nrm(ks[19], (DEPTH, CONV_W, DFF), CONV_W ** -0.5),
        'ffn_conv_b': nrm(ks[20], (DEPTH, DFF), 0.02),
        'ffn_w_down': nrm(ks[21], (DEPTH, DFF, D_MODEL), DFF ** -0.5),
        'final_norm_g': gain(ks[22], (D_MODEL,)),
    }


def reference(x, attn_norm_g, ffn_norm_g, mlstm_w_in, mlstm_b_gate, mlstm_head_g, mlstm_w_out,
              kv_norm_g, w_kv, diff_w_q, diff_lambda_q1, diff_lambda_k1, diff_lambda_q2, diff_lambda_k2,
              diff_subln_g, diff_w_out, ffn_w_up, ffn_conv_w, ffn_conv_b, ffn_w_down, final_norm_g):
    k_sh, v_sh = None, None
    for l in range(DEPTH):
        if l == N_A:
            k_sh, v_sh = shared_kv(x, kv_norm_g, w_kv)
        h = rms_norm(x, attn_norm_g[l])
        if l < N_A:
            x = x + mlstm_mixer(h, mlstm_w_in[l], mlstm_b_gate[l], mlstm_head_g[l], mlstm_w_out[l])
        else:
            j = l - N_A
            x = x + diff_mixer(h, k_sh, v_sh, diff_w_q[j], diff_lambda_q1[j], diff_lambda_k1[j],
                               diff_lambda_q2[j], diff_lambda_k2[j], diff_subln_g[j], diff_w_out[j], l)
        x = x + conv_ffn(rms_norm(x, ffn_norm_g[l]), ffn_w_up[l], ffn_conv_w[l], ffn_conv_b[l], ffn_w_down[l])
    return rms_norm(x, final_norm_g)
```

```python
import functools
import math

import jax
import jax.numpy as jnp
from jax import lax
from jax.experimental import pallas as pl
from jax.experimental.pallas import tpu as pltpu

RMS_EPS = 1e-6
GATE_SOFTCAP = 15.0
CONV_TAPS = 3

LANES = 128
SUBLANES = 8
VMEM_LIMIT_BYTES = 56 * 1024 * 1024

PROJ_ROWS = 512
FFN_ROWS = 512
FFN_COLS = 256
MLSTM_CHUNK = 256
ATTN_BLOCK = 256

_BF16 = jnp.bfloat16
_F32 = jnp.float32
_NT = (((1,), (1,)), ((), ()))


def _dot(a, b):
    return jnp.dot(a, b, preferred_element_type=_F32)


def _dot_nt(a, b):
    return lax.dot_general(a, b, _NT, preferred_element_type=_F32)


def _rms_rows(x, g):
    ms = jnp.mean(x * x, axis=-1, keepdims=True)
    return x * lax.rsqrt(ms + RMS_EPS) * g


def _resident(shape):
    nd = len(shape)
    return pl.BlockSpec(shape, lambda *_: (0,) * nd, pipeline_mode=pl.Buffered(1))


def _params(*sem):
    return pltpu.CompilerParams(dimension_semantics=sem, vmem_limit_bytes=VMEM_LIMIT_BYTES)


def _mlstm_proj_kernel(x_ref, g_ref, wnn_ref, wkt_ref, wgt_ref, bg_ref,
                       q_ref, v_ref, o_ref, kt_ref, gates_ref, *, nq, nv, k_scale, n_heads):
    hb = _rms_rows(x_ref[0], g_ref[...]).astype(_BF16)
    q_ref[0] = _dot(hb, wnn_ref[:, 0:nq]).astype(_BF16)
    v_ref[0] = _dot(hb, wnn_ref[:, nq:nq + nv]).astype(_BF16)
    o_ref[0] = _dot(hb, wnn_ref[:, nq + nv:nq + 2 * nv])
    kt_ref[0] = (_dot_nt(wkt_ref[...], hb) * k_scale).astype(_BF16)
    gt = _dot_nt(wgt_ref[...], hb)[0:2 * n_heads] + bg_ref[...]
    gt = GATE_SOFTCAP * jnp.tanh(gt / GATE_SOFTCAP)
    row = lax.broadcasted_iota(jnp.int32, gt.shape, 0)
    gates_ref[0] = jnp.where(row < n_heads, gt, jax.nn.log_sigmoid(gt))


def _mlstm_proj(x, g, w_in, b_gate, n_heads, dk, dv):
    b, s, d = x.shape
    tm = PROJ_ROWS
    nq, nv = n_heads * dk, n_heads * dv
    w_q = w_in[:, 0:nq]
    w_k = w_in[:, nq:2 * nq]
    w_v = w_in[:, 2 * nq:2 * nq + nv]
    w_o = w_in[:, 2 * nq + nv:2 * nq + 2 * nv]
    w_g = w_in[:, 2 * nq + 2 * nv:]
    wnn = jnp.concatenate([w_q, w_v, w_o], axis=1).astype(_BF16)
    wkt = w_k.T.astype(_BF16)
    ng = 2 * n_heads
    ng_pad = 2 * SUBLANES
    wgt = jnp.zeros((ng_pad, d), _BF16).at[0:ng].set(w_g.T.astype(_BF16))
    bg = b_gate.astype(_F32).reshape(ng, 1)
    kern = functools.partial(_mlstm_proj_kernel, nq=nq, nv=nv, k_scale=dk ** -0.5, n_heads=n_heads)
    return pl.pallas_call(
        kern,
        grid=(b, s // tm),
        in_specs=[
            pl.BlockSpec((1, tm, d), lambda i, j: (i, j, 0)),
            _resident((1, d)),
            _resident(wnn.shape),
            _resident(wkt.shape),
            _resident(wgt.shape),
            _resident(bg.shape),
        ],
        out_specs=[
            pl.BlockSpec((1, tm, nq), lambda i, j: (i, j, 0)),
            pl.BlockSpec((1, tm, nv), lambda i, j: (i, j, 0)),
            pl.BlockSpec((1, tm, nv), lambda i, j: (i, j, 0)),
            pl.BlockSpec((1, nq, tm), lambda i, j: (i, 0, j)),
            pl.BlockSpec((1, ng, tm), lambda i, j: (i, 0, j)),
        ],
        out_shape=[
            jax.ShapeDtypeStruct((b, s, nq), _BF16),
            jax.ShapeDtypeStruct((b, s, nv), _BF16),
            jax.ShapeDtypeStruct((b, s, nv), _F32),
            jax.ShapeDtypeStruct((b, nq, s), _BF16),
            jax.ShapeDtypeStruct((b, ng, s), _F32),
        ],
        compiler_params=_params("parallel", "parallel"),
        name="mlstm_proj",
    )(x, g.reshape(1, d), wnn, wkt, wgt, bg)


def _cumsum_lanes(x):
    n = x.shape[-1]
    lane = lax.broadcasted_iota(jnp.int32, x.shape, x.ndim - 1)
    shift = 1
    while shift < n:
        x = x + jnp.where(lane >= shift, pltpu.roll(x, shift, x.ndim - 1), 0.0)
        shift *= 2
    return x


def _mlstm_kernel(q_ref, kt_ref, v_ref, gates_ref, o_ref, hg_ref, y_ref,
                  ct_ref, nt_ref, m_ref, *, n_heads, dk, dv):
    c = pl.program_id(1)
    lc = q_ref.shape[1]

    @pl.when(c == 0)
    def _():
        ct_ref[...] = jnp.zeros_like(ct_ref)
        nt_ref[...] = jnp.zeros_like(nt_ref)
        m_ref[...] = jnp.zeros_like(m_ref)

    gates = gates_ref[0]
    a_rows = _cumsum_lanes(gates[n_heads:2 * n_heads])
    t_idx = lax.broadcasted_iota(jnp.int32, (lc, lc), 0)
    s_idx = lax.broadcasted_iota(jnp.int32, (lc, lc), 1)
    causal = t_idx >= s_idx
    ones_b = jnp.ones((lc, LANES), _BF16)

    for h in range(n_heads):
        i_row = gates[h:h + 1]
        f_row = gates[n_heads + h:n_heads + h + 1]
        a_row = a_rows[h:h + 1]
        a_last = a_row[:, lc - 1:lc]
        m_prev = m_ref[h][0:1, 0:1]
        a_col = jnp.sum(jnp.where(causal, f_row, 0.0), axis=1, keepdims=True)
        d_row = i_row - a_row
        log_intra = jnp.where(causal, a_col + d_row, -jnp.inf)
        log_inter = a_col + m_prev
        m_t = jnp.maximum(log_inter, jnp.max(log_intra, axis=1, keepdims=True))
        w_intra = jnp.exp(log_intra - m_t)
        w_inter = jnp.exp(log_inter - m_t)

        q = q_ref[0, :, h * dk:(h + 1) * dk]
        kt = kt_ref[0, h * dk:(h + 1) * dk, :]
        v = v_ref[0, :, h * dv:(h + 1) * dv]
        ct = ct_ref[h]
        nt = nt_ref[h]

        sc = (_dot(q, kt) * w_intra).astype(_BF16)
        num = _dot(sc, v) + w_inter * _dot(q, ct.astype(_BF16))
        den = _dot(sc, ones_b) + w_inter * _dot(q, nt.astype(_BF16))
        inv = 1.0 / jnp.maximum(jnp.abs(den), jnp.exp(-m_t))
        hs = num * jnp.concatenate([inv] * (dv // LANES), axis=1)

        log_src = a_last + d_row
        m_new = jnp.maximum(a_last + m_prev, jnp.max(log_src, axis=1, keepdims=True))
        w_src = jnp.exp(log_src - m_new)
        decay = jnp.exp(a_last + m_prev - m_new)
        kw = (kt.astype(_F32) * w_src).astype(_BF16)
        ct_ref[h] = decay * ct + _dot(kw, v)
        nt_ref[h] = decay * nt + _dot(kw, ones_b)
        m_ref[h] = jnp.broadcast_to(m_new, m_ref.shape[1:])

        hn = _rms_rows(hs, hg_ref[:, h * dv:(h + 1) * dv])
        gate = jax.nn.sigmoid(o_ref[0, :, h * dv:(h + 1) * dv])
        y_ref[0, :, h * dv:(h + 1) * dv] = (hn * gate).astype(_BF16)


def _mlstm(q, kt, v, gates, o, head_g, n_heads, dk, dv):
    b, s, _ = q.shape
    lc = MLSTM_CHUNK
    ng = 2 * n_heads
    kern = functools.partial(_mlstm_kernel, n_heads=n_heads, dk=dk, dv=dv)
    return pl.pallas_call(
        kern,
        grid=(b, s // lc),
        in_specs=[
            pl.BlockSpec((1, lc, n_heads * dk), lambda i, j: (i, j, 0)),
            pl.BlockSpec((1, n_heads * dk, lc), lambda i, j: (i, 0, j)),
            pl.BlockSpec((1, lc, n_heads * dv), lambda i, j: (i, j, 0)),
            pl.BlockSpec((1, ng, lc), lambda i, j: (i, 0, j)),
            pl.BlockSpec((1, lc, n_heads * dv), lambda i, j: (i, j, 0)),
            _resident((1, n_heads * dv)),
        ],
        out_specs=pl.BlockSpec((1, lc, n_heads * dv), lambda i, j: (i, j, 0)),
        out_shape=jax.ShapeDtypeStruct((b, s, n_heads * dv), _BF16),
        scratch_shapes=[
            pltpu.VMEM((n_heads, dk, dv), _F32),
            pltpu.VMEM((n_heads, dk, LANES), _F32),
            pltpu.VMEM((n_heads, SUBLANES, LANES), _F32),
        ],
        compiler_params=_params("parallel", "arbitrary"),
        name="mlstm_chunk",
    )(q, kt, v, gates, o, head_g.reshape(1, n_heads * dv).astype(_F32))


def _out_ffn_kernel(*refs, dff, final_norm):
    if final_norm:
        (y_ref, x_ref, wout_ref, g_ref, wup_ref, cw_ref, cb_ref, wdn_ref, fg_ref,
         out_ref, act_ref, gbuf_ref, carry_ref) = refs
    else:
        (y_ref, x_ref, wout_ref, g_ref, wup_ref, cw_ref, cb_ref, wdn_ref,
         out_ref, act_ref, gbuf_ref, carry_ref) = refs
    tm = x_ref.shape[1]
    ck = gbuf_ref.shape[1]

    @pl.when(pl.program_id(1) == 0)
    def _():
        carry_ref[...] = jnp.zeros_like(carry_ref)

    x1 = x_ref[0] + _dot(y_ref[0], wout_ref[...])
    out_ref[0] = x1
    hb = _rms_rows(x1, g_ref[...]).astype(_BF16)

    for c in range(dff // ck):
        lo = c * ck
        gj = _dot(hb, wup_ref[:, lo:lo + ck])
        uj = _dot(hb, wup_ref[:, dff + lo:dff + lo + ck])
        gbuf_ref[0:SUBLANES, :] = carry_ref[:, lo:lo + ck]
        gbuf_ref[SUBLANES:SUBLANES + tm, :] = gj
        carry_ref[:, lo:lo + ck] = gj[tm - SUBLANES:tm, :]
        g1 = gbuf_ref[SUBLANES - 1:SUBLANES - 1 + tm, :]
        g2 = gbuf_ref[SUBLANES - 2:SUBLANES - 2 + tm, :]
        gc = (cb_ref[:, lo:lo + ck]
              + cw_ref[0:1, lo:lo + ck] * g2
              + cw_ref[1:2, lo:lo + ck] * g1
              + cw_ref[2:3, lo:lo + ck] * gj)
        act_ref[:, lo:lo + ck] = (gc * jax.nn.sigmoid(gc) * uj).astype(_BF16)

    x2 = out_ref[0] + _dot(act_ref[...], wdn_ref[...])
    if final_norm:
        x2 = _rms_rows(x2, fg_ref[...])
    out_ref[0] = x2


def _out_ffn(y, x, w_out, norm_g, w_up, conv_w, conv_b, w_down, final_g=None):
    b, s, d = x.shape
    dy = y.shape[-1]
    dff = w_down.shape[0]
    tm = FFN_ROWS
    ck = FFN_COLS
    final_norm = final_g is not None
    args = [y, x, w_out.astype(_BF16), norm_g.reshape(1, d).astype(_F32), w_up.astype(_BF16),
            conv_w.astype(_F32), conv_b.reshape(1, dff).astype(_F32), w_down.astype(_BF16)]
    in_specs = [
        pl.BlockSpec((1, tm, dy), lambda i, j: (i, j, 0)),
        pl.BlockSpec((1, tm, d), lambda i, j: (i, j, 0)),
        _resident((dy, d)),
        _resident((1, d)),
        _resident((d, 2 * dff)),
        _resident((CONV_TAPS, dff)),
        _resident((1, dff)),
        _resident((dff, d)),
    ]
    if final_norm:
        args.append(final_g.reshape(1, d).astype(_F32))
        in_specs.append(_resident((1, d)))
    kern = functools.partial(_out_ffn_kernel, dff=dff, final_norm=final_norm)
    return pl.pallas_call(
        kern,
        grid=(b, s // tm),
        in_specs=in_specs,
        out_specs=pl.BlockSpec((1, tm, d), lambda i, j: (i, j, 0)),
        out_shape=jax.ShapeDtypeStruct((b, s, d), _F32),
        scratch_shapes=[
            pltpu.VMEM((tm, dff), _BF16),
            pltpu.VMEM((SUBLANES + tm, ck), _F32),
            pltpu.VMEM((SUBLANES, dff), _F32),
        ],
        compiler_params=_params("parallel", "arbitrary"),
        name="out_ffn_final" if final_norm else "out_ffn",
    )(*args)


def _kv_proj_kernel(x_ref, g_ref, wk_ref, wvt_ref, k_ref, vt_ref):
    hb = _rms_rows(x_ref[0], g_ref[...]).astype(_BF16)
    k_ref[0] = _dot(hb, wk_ref[...]).astype(_BF16)
    vt = _dot_nt(wvt_ref[...], hb).astype(_BF16)
    tk = vt_ref.shape[3]
    for t in range(vt_ref.shape[1]):
        vt_ref[0, t] = vt[:, t * tk:(t + 1) * tk]


def _kv_proj(x, g, w_kv, qk):
    b, s, d = x.shape
    tm = PROJ_ROWS
    tk = ATTN_BLOCK
    nv = w_kv.shape[1] - qk
    wk = w_kv[:, :qk].astype(_BF16)
    wvt = w_kv[:, qk:].T.astype(_BF16)
    return pl.pallas_call(
        _kv_proj_kernel,
        grid=(b, s // tm),
        in_specs=[
            pl.BlockSpec((1, tm, d), lambda i, j: (i, j, 0)),
            _resident((1, d)),
            _resident(wk.shape),
            _resident(wvt.shape),
        ],
        out_specs=[
            pl.BlockSpec((1, tm, qk), lambda i, j: (i, j, 0)),
            pl.BlockSpec((1, tm // tk, nv, tk), lambda i, j: (i, j, 0, 0)),
        ],
        out_shape=[
            jax.ShapeDtypeStruct((b, s, qk), _BF16),
            jax.ShapeDtypeStruct((b, s // tk, nv, tk), _BF16),
        ],
        compiler_params=_params("parallel", "parallel"),
        name="kv_proj",
    )(x, g.reshape(1, d).astype(_F32), wk, wvt)


def _q_proj_kernel(x_ref, g_ref, wqt_ref, qt_ref, *, scale):
    hb = _rms_rows(x_ref[0], g_ref[...]).astype(_BF16)
    qt_ref[0] = (_dot_nt(wqt_ref[...], hb) * scale).astype(_BF16)


def _q_proj(x, g, w_q, scale):
    b, s, d = x.shape
    tm = PROJ_ROWS
    qk = w_q.shape[1]
    wqt = w_q.T.astype(_BF16)
    return pl.pallas_call(
        functools.partial(_q_proj_kernel, scale=scale),
        grid=(b, s // tm),
        in_specs=[
            pl.BlockSpec((1, tm, d), lambda i, j: (i, j, 0)),
            _resident((1, d)),
            _resident(wqt.shape),
        ],
        out_specs=pl.BlockSpec((1, qk, tm), lambda i, j: (i, 0, j)),
        out_shape=jax.ShapeDtypeStruct((b, qk, s), _BF16),
        compiler_params=_params("parallel", "parallel"),
        name="q_proj",
    )(x, g.reshape(1, d).astype(_F32), wqt)


def _diff_attn_kernel(qt_ref, k_ref, vt_ref, lq1_ref, lk1_ref, lq2_ref, lk2_ref, g_ref, o_ref,
                      *, hd, lam_init):
    qi = pl.program_id(2)
    tq = qt_ref.shape[2]
    tk = vt_ref.shape[3]
    dv = vt_ref.shape[2]
    qt = qt_ref[0]
    row = lax.broadcasted_iota(jnp.int32, qt.shape, 0)
    zero = jnp.zeros_like(qt)
    qts = (jnp.where(row < hd, qt, zero), jnp.where(row >= hd, qt, zero))

    def block(j, carry, masked):
        k = k_ref[0, pl.ds(pl.multiple_of(j * tk, tk), tk), :]
        vt = vt_ref[0, j]
        if masked:
            kpos = lax.broadcasted_iota(jnp.int32, (tk, tq), 0)
            qpos = lax.broadcasted_iota(jnp.int32, (tk, tq), 1)
            keep = qpos >= kpos
        out = []
        for c in range(2):
            m, l, acc = carry[c]
            sc = _dot(k, qts[c])
            if masked:
                sc = jnp.where(keep, sc, -jnp.inf)
            m_new = jnp.maximum(m, jnp.max(sc, axis=0, keepdims=True))
            alpha = jnp.exp(m - m_new)
            p = jnp.exp(sc - m_new)
            l = alpha * l + jnp.sum(p, axis=0, keepdims=True)
            acc = alpha * acc + _dot(vt, p.astype(_BF16))
            out.append((m_new, l, acc))
        return tuple(out)

    init1 = (jnp.full((1, tq), -jnp.inf, _F32), jnp.zeros((1, tq), _F32), jnp.zeros((dv, tq), _F32))
    carry = lax.fori_loop(0, qi, lambda j, cr: block(j, cr, False), (init1, init1))
    (_, l1, a1), (_, l2, a2) = block(qi, carry, True)

    lam = (jnp.exp(jnp.sum(lq1_ref[...] * lk1_ref[...], axis=1, keepdims=True))
           - jnp.exp(jnp.sum(lq2_ref[...] * lk2_ref[...], axis=1, keepdims=True)) + lam_init)
    o = a1 / l1 - lam * (a2 / l2)
    ms = jnp.mean(o * o, axis=0, keepdims=True)
    o = o * lax.rsqrt(ms + RMS_EPS) * g_ref[...] * (1.0 - lam_init)
    o_ref[0] = o.T.astype(_BF16)


def _diff_attn(qt, k, vtb, lq1, lk1, lq2, lk2, subln_g, n_heads, hd, lam_init):
    b, qk, s = qt.shape
    tq = ATTN_BLOCK
    nkb, nv, tk = vtb.shape[1:]
    dv = nv // n_heads
    lam_args = [a.reshape(1, hd).astype(_F32) for a in (lq1, lk1, lq2, lk2)]
    kern = functools.partial(_diff_attn_kernel, hd=hd, lam_init=lam_init)
    return pl.pallas_call(
        kern,
        grid=(b, n_heads, s // tq),
        in_specs=[
            pl.BlockSpec((1, 2 * hd, tq), lambda i, h, j: (i, h, j)),
            pl.BlockSpec((1, s, 2 * hd), lambda i, h, j: (i, 0, h)),
            pl.BlockSpec((1, nkb, dv, tk), lambda i, h, j: (i, 0, h, 0)),
            _resident((1, hd)), _resident((1, hd)), _resident((1, hd)), _resident((1, hd)),
            _resident((dv, 1)),
        ],
        out_specs=pl.BlockSpec((1, tq, dv), lambda i, h, j: (i, j, h)),
        out_shape=jax.ShapeDtypeStruct((b, s, nv), _BF16),
        compiler_params=_params("parallel", "parallel", "arbitrary"),
        name="diff_attn",
    )(qt, k, vtb, *lam_args, subln_g.reshape(dv, 1).astype(_F32))


def kernel(x, attn_norm_g, ffn_norm_g, mlstm_w_in, mlstm_b_gate, mlstm_head_g, mlstm_w_out, kv_norm_g, w_kv, diff_w_q, diff_lambda_q1, diff_lambda_k1, diff_lambda_q2, diff_lambda_k2, diff_subln_g, diff_w_out, ffn_w_up, ffn_conv_w, ffn_conv_b, ffn_w_down, final_norm_g):
    depth = ffn_w_up.shape[0]
    n_a = mlstm_w_in.shape[0]
    ha, dv_a = mlstm_head_g.shape[1:]
    dk_a = (mlstm_w_in.shape[2] - 2 * ha * dv_a - 2 * ha) // (2 * ha)
    hd_b = diff_lambda_q1.shape[1]
    qk_b = diff_w_q.shape[2]
    hb = qk_b // (2 * hd_b)

    k_sh = vt_sh = None
    for l in range(depth):
        if l == n_a:
            k_sh, vt_sh = _kv_proj(x, kv_norm_g, w_kv, qk_b)
        if l < n_a:
            q, v, o, kt, gates = _mlstm_proj(x, attn_norm_g[l], mlstm_w_in[l], mlstm_b_gate[l], ha, dk_a, dv_a)
            y = _mlstm(q, kt, v, gates, o, mlstm_head_g[l], ha, dk_a, dv_a)
            w_out = mlstm_w_out[l]
        else:
            j = l - n_a
            lam_init = 0.8 - 0.6 * math.exp(-0.3 * l)
            qt = _q_proj(x, attn_norm_g[l], diff_w_q[j], hd_b ** -0.5)
            y = _diff_attn(qt, k_sh, vt_sh, diff_lambda_q1[j], diff_lambda_k1[j], diff_lambda_q2[j],
                           diff_lambda_k2[j], diff_subln_g[j], hb, hd_b, lam_init)
            w_out = diff_w_out[j]
        x = _out_ffn(y, x, w_out, ffn_norm_g[l], ffn_w_up[l], ffn_conv_w[l], ffn_conv_b[l], ffn_w_down[l],
                     final_g=final_norm_g if l == depth - 1 else None)
    return x
```

```python
import functools
import math

import jax
import jax.numpy as jnp
from jax import lax
from jax.experimental import pallas as pl
from jax.experimental.pallas import tpu as pltpu

RMS_EPS = 1e-6
GATE_SOFTCAP = 15.0
CONV_TAPS = 3

LANES = 128
SUBLANES = 8
VMEM_LIMIT_BYTES = 56 * 1024 * 1024

PROJ_ROWS = 512
FFN_ROWS = 512
FFN_COLS = 256
MLSTM_CHUNK = 256
ATTN_BLOCK = 512
ATTN_HEADS_PER_STEP = 2
ATTN_DEN_ROWS = 16

_BF16 = jnp.bfloat16
_F32 = jnp.float32
_NT = (((1,), (1,)), ((), ()))


def _dot(a, b):
    return jnp.dot(a, b, preferred_element_type=_F32)


def _dot_nt(a, b):
    return lax.dot_general(a, b, _NT, preferred_element_type=_F32)


def _rms_rows(x, g):
    ms = jnp.mean(x * x, axis=-1, keepdims=True)
    return x * lax.rsqrt(ms + RMS_EPS) * g


def _resident(shape):
    nd = len(shape)
    return pl.BlockSpec(shape, lambda *_: (0,) * nd, pipeline_mode=pl.Buffered(1))


def _params(*sem):
    return pltpu.CompilerParams(dimension_semantics=sem, vmem_limit_bytes=VMEM_LIMIT_BYTES)


def _mlstm_proj_kernel(x_ref, g_ref, wnn_ref, wkt_ref, wgt_ref, bg_ref,
                       q_ref, v_ref, o_ref, kt_ref, gates_ref, *, nq, nv, k_scale, n_heads):
    hb = _rms_rows(x_ref[0], g_ref[...]).astype(_BF16)
    q_ref[0] = _dot(hb, wnn_ref[:, 0:nq]).astype(_BF16)
    v_ref[0] = _dot(hb, wnn_ref[:, nq:nq + nv]).astype(_BF16)
    o_ref[0] = _dot(hb, wnn_ref[:, nq + nv:nq + 2 * nv])
    kt_ref[0] = (_dot_nt(wkt_ref[...], hb) * k_scale).astype(_BF16)
    gt = _dot_nt(wgt_ref[...], hb)[0:2 * n_heads] + bg_ref[...]
    gt = GATE_SOFTCAP * jnp.tanh(gt / GATE_SOFTCAP)
    row = lax.broadcasted_iota(jnp.int32, gt.shape, 0)
    gates_ref[0] = jnp.where(row < n_heads, gt, jax.nn.log_sigmoid(gt))


def _mlstm_proj(x, g, w_in, b_gate, n_heads, dk, dv):
    b, s, d = x.shape
    tm = PROJ_ROWS
    nq, nv = n_heads * dk, n_heads * dv
    w_q = w_in[:, 0:nq]
    w_k = w_in[:, nq:2 * nq]
    w_v = w_in[:, 2 * nq:2 * nq + nv]
    w_o = w_in[:, 2 * nq + nv:2 * nq + 2 * nv]
    w_g = w_in[:, 2 * nq + 2 * nv:]
    wnn = jnp.concatenate([w_q, w_v, w_o], axis=1).astype(_BF16)
    wkt = w_k.T.astype(_BF16)
    ng = 2 * n_heads
    ng_pad = 2 * SUBLANES
    wgt = jnp.zeros((ng_pad, d), _BF16).at[0:ng].set(w_g.T.astype(_BF16))
    bg = b_gate.astype(_F32).reshape(ng, 1)
    kern = functools.partial(_mlstm_proj_kernel, nq=nq, nv=nv, k_scale=dk ** -0.5, n_heads=n_heads)
    return pl.pallas_call(
        kern,
        grid=(b, s // tm),
        in_specs=[
            pl.BlockSpec((1, tm, d), lambda i, j: (i, j, 0)),
            _resident((1, d)),
            _resident(wnn.shape),
            _resident(wkt.shape),
            _resident(wgt.shape),
            _resident(bg.shape),
        ],
        out_specs=[
            pl.BlockSpec((1, tm, nq), lambda i, j: (i, j, 0)),
            pl.BlockSpec((1, tm, nv), lambda i, j: (i, j, 0)),
            pl.BlockSpec((1, tm, nv), lambda i, j: (i, j, 0)),
            pl.BlockSpec((1, nq, tm), lambda i, j: (i, 0, j)),
            pl.BlockSpec((1, ng, tm), lambda i, j: (i, 0, j)),
        ],
        out_shape=[
            jax.ShapeDtypeStruct((b, s, nq), _BF16),
            jax.ShapeDtypeStruct((b, s, nv), _BF16),
            jax.ShapeDtypeStruct((b, s, nv), _F32),
            jax.ShapeDtypeStruct((b, nq, s), _BF16),
            jax.ShapeDtypeStruct((b, ng, s), _F32),
        ],
        compiler_params=_params("parallel", "parallel"),
        name="mlstm_proj",
    )(x, g.reshape(1, d), wnn, wkt, wgt, bg)


def _cumsum_lanes(x):
    n = x.shape[-1]
    lane = lax.broadcasted_iota(jnp.int32, x.shape, x.ndim - 1)
    shift = 1
    while shift < n:
        x = x + jnp.where(lane >= shift, pltpu.roll(x, shift, x.ndim - 1), 0.0)
        shift *= 2
    return x


def _mlstm_kernel(q_ref, kt_ref, v_ref, gates_ref, o_ref, hg_ref, y_ref,
                  ct_ref, nt_ref, m_ref, *, n_heads, dk, dv):
    c = pl.program_id(1)
    lc = q_ref.shape[1]

    @pl.when(c == 0)
    def _():
        ct_ref[...] = jnp.zeros_like(ct_ref)
        nt_ref[...] = jnp.zeros_like(nt_ref)
        m_ref[...] = jnp.zeros_like(m_ref)

    gates = gates_ref[0]
    a_rows = _cumsum_lanes(gates[n_heads:2 * n_heads])
    t_idx = lax.broadcasted_iota(jnp.int32, (lc, lc), 0)
    s_idx = lax.broadcasted_iota(jnp.int32, (lc, lc), 1)
    causal = t_idx >= s_idx
    ones_b = jnp.ones((lc, LANES), _BF16)

    for h in range(n_heads):
        i_row = gates[h:h + 1]
        f_row = gates[n_heads + h:n_heads + h + 1]
        a_row = a_rows[h:h + 1]
        a_last = a_row[:, lc - 1:lc]
        m_prev = m_ref[h][0:1, 0:1]
        a_col = jnp.sum(jnp.where(causal, f_row, 0.0), axis=1, keepdims=True)
        d_row = i_row - a_row
        log_intra = jnp.where(causal, a_col + d_row, -jnp.inf)
        log_inter = a_col + m_prev
        m_t = jnp.maximum(log_inter, jnp.max(log_intra, axis=1, keepdims=True))
        w_intra = jnp.exp(log_intra - m_t)
        w_inter = jnp.exp(log_inter - m_t)

        q = q_ref[0, :, h * dk:(h + 1) * dk]
        kt = kt_ref[0, h * dk:(h + 1) * dk, :]
        v = v_ref[0, :, h * dv:(h + 1) * dv]
        ct = ct_ref[h]
        nt = nt_ref[h]

        sc = (_dot(q, kt) * w_intra).astype(_BF16)
        num = _dot(sc, v) + w_inter * _dot(q, ct.astype(_BF16))
        den = _dot(sc, ones_b) + w_inter * _dot(q, nt.astype(_BF16))
        inv = 1.0 / jnp.maximum(jnp.abs(den), jnp.exp(-m_t))
        hs = num * jnp.concatenate([inv] * (dv // LANES), axis=1)

        log_src = a_last + d_row
        m_new = jnp.maximum(a_last + m_prev, jnp.max(log_src, axis=1, keepdims=True))
        w_src = jnp.exp(log_src - m_new)
        decay = jnp.exp(a_last + m_prev - m_new)
        kw = (kt.astype(_F32) * w_src).astype(_BF16)
        ct_ref[h] = decay * ct + _dot(kw, v)
        nt_ref[h] = decay * nt + _dot(kw, ones_b)
        m_ref[h] = jnp.broadcast_to(m_new, m_ref.shape[1:])

        hn = _rms_rows(hs, hg_ref[:, h * dv:(h + 1) * dv])
        gate = jax.nn.sigmoid(o_ref[0, :, h * dv:(h + 1) * dv])
        y_ref[0, :, h * dv:(h + 1) * dv] = (hn * gate).astype(_BF16)


def _mlstm(q, kt, v, gates, o, head_g, n_heads, dk, dv):
    b, s, _ = q.shape
    lc = MLSTM_CHUNK
    ng = 2 * n_heads
    kern = functools.partial(_mlstm_kernel, n_heads=n_heads, dk=dk, dv=dv)
    return pl.pallas_call(
        kern,
        grid=(b, s // lc),
        in_specs=[
            pl.BlockSpec((1, lc, n_heads * dk), lambda i, j: (i, j, 0)),
            pl.BlockSpec((1, n_heads * dk, lc), lambda i, j: (i, 0, j)),
            pl.BlockSpec((1, lc, n_heads * dv), lambda i, j: (i, j, 0)),
            pl.BlockSpec((1, ng, lc), lambda i, j: (i, 0, j)),
            pl.BlockSpec((1, lc, n_heads * dv), lambda i, j: (i, j, 0)),
            _resident((1, n_heads * dv)),
        ],
        out_specs=pl.BlockSpec((1, lc, n_heads * dv), lambda i, j: (i, j, 0)),
        out_shape=jax.ShapeDtypeStruct((b, s, n_heads * dv), _BF16),
        scratch_shapes=[
            pltpu.VMEM((n_heads, dk, dv), _F32),
            pltpu.VMEM((n_heads, dk, LANES), _F32),
            pltpu.VMEM((n_heads, SUBLANES, LANES), _F32),
        ],
        compiler_params=_params("parallel", "arbitrary"),
        name="mlstm_chunk",
    )(q, kt, v, gates, o, head_g.reshape(1, n_heads * dv).astype(_F32))


def _out_ffn_kernel(*refs, dff, final_norm):
    if final_norm:
        (y_ref, x_ref, wout_ref, g_ref, wup_ref, cw_ref, cb_ref, wdn_ref, fg_ref,
         out_ref, act_ref, gbuf_ref, carry_ref) = refs
    else:
        (y_ref, x_ref, wout_ref, g_ref, wup_ref, cw_ref, cb_ref, wdn_ref,
         out_ref, act_ref, gbuf_ref, carry_ref) = refs
    tm = x_ref.shape[1]
    ck = gbuf_ref.shape[1]

    @pl.when(pl.program_id(1) == 0)
    def _():
        carry_ref[...] = jnp.zeros_like(carry_ref)

    x1 = x_ref[0] + _dot(y_ref[0], wout_ref[...])
    out_ref[0] = x1
    hb = _rms_rows(x1, g_ref[...]).astype(_BF16)

    for c in range(dff // ck):
        lo = c * ck
        gj = _dot(hb, wup_ref[:, lo:lo + ck])
        uj = _dot(hb, wup_ref[:, dff + lo:dff + lo + ck])
        gbuf_ref[0:SUBLANES, :] = carry_ref[:, lo:lo + ck]
        gbuf_ref[SUBLANES:SUBLANES + tm, :] = gj
        carry_ref[:, lo:lo + ck] = gj[tm - SUBLANES:tm, :]
        g1 = gbuf_ref[SUBLANES - 1:SUBLANES - 1 + tm, :]
        g2 = gbuf_ref[SUBLANES - 2:SUBLANES - 2 + tm, :]
        gc = (cb_ref[:, lo:lo + ck]
              + cw_ref[0:1, lo:lo + ck] * g2
              + cw_ref[1:2, lo:lo + ck] * g1
              + cw_ref[2:3, lo:lo + ck] * gj)
        act_ref[:, lo:lo + ck] = (gc * jax.nn.sigmoid(gc) * uj).astype(_BF16)

    x2 = out_ref[0] + _dot(act_ref[...], wdn_ref[...])
    if final_norm:
        x2 = _rms_rows(x2, fg_ref[...])
    out_ref[0] = x2


def _out_ffn(y, x, w_out, norm_g, w_up, conv_w, conv_b, w_down, final_g=None):
    b, s, d = x.shape
    dy = y.shape[-1]
    dff = w_down.shape[0]
    tm = FFN_ROWS
    ck = FFN_COLS
    final_norm = final_g is not None
    args = [y, x, w_out.astype(_BF16), norm_g.reshape(1, d).astype(_F32), w_up.astype(_BF16),
            conv_w.astype(_F32), conv_b.reshape(1, dff).astype(_F32), w_down.astype(_BF16)]
    in_specs = [
        pl.BlockSpec((1, tm, dy), lambda i, j: (i, j, 0)),
        pl.BlockSpec((1, tm, d), lambda i, j: (i, j, 0)),
        _resident((dy, d)),
        _resident((1, d)),
        _resident((d, 2 * dff)),
        _resident((CONV_TAPS, dff)),
        _resident((1, dff)),
        _resident((dff, d)),
    ]
    if final_norm:
        args.append(final_g.reshape(1, d).astype(_F32))
        in_specs.append(_resident((1, d)))
    kern = functools.partial(_out_ffn_kernel, dff=dff, final_norm=final_norm)
    return pl.pallas_call(
        kern,
        grid=(b, s // tm),
        in_specs=in_specs,
        out_specs=pl.BlockSpec((1, tm, d), lambda i, j: (i, j, 0)),
        out_shape=jax.ShapeDtypeStruct((b, s, d), _F32),
        scratch_shapes=[
            pltpu.VMEM((tm, dff), _BF16),
            pltpu.VMEM((SUBLANES + tm, ck), _F32),
            pltpu.VMEM((SUBLANES, dff), _F32),
        ],
        compiler_params=_params("parallel", "arbitrary"),
        name="out_ffn_final" if final_norm else "out_ffn",
    )(*args)


def _kv_proj_kernel(x_ref, g_ref, wk_ref, wvt_ref, k_ref, vt_ref):
    hb = _rms_rows(x_ref[0], g_ref[...]).astype(_BF16)
    k_ref[0] = _dot(hb, wk_ref[...]).astype(_BF16)
    vt = _dot_nt(wvt_ref[...], hb).astype(_BF16)
    tk = vt_ref.shape[3]
    for t in range(vt_ref.shape[1]):
        vt_ref[0, t] = vt[:, t * tk:(t + 1) * tk]


def _kv_proj(x, g, w_kv, qk):
    b, s, d = x.shape
    tm = PROJ_ROWS
    tk = ATTN_BLOCK
    nv = w_kv.shape[1] - qk
    wk = w_kv[:, :qk].astype(_BF16)
    wvt = w_kv[:, qk:].T.astype(_BF16)
    return pl.pallas_call(
        _kv_proj_kernel,
        grid=(b, s // tm),
        in_specs=[
            pl.BlockSpec((1, tm, d), lambda i, j: (i, j, 0)),
            _resident((1, d)),
            _resident(wk.shape),
            _resident(wvt.shape),
        ],
        out_specs=[
            pl.BlockSpec((1, tm, qk), lambda i, j: (i, j, 0)),
            pl.BlockSpec((1, tm // tk, nv, tk), lambda i, j: (i, j, 0, 0)),
        ],
        out_shape=[
            jax.ShapeDtypeStruct((b, s, qk), _BF16),
            jax.ShapeDtypeStruct((b, s // tk, nv, tk), _BF16),
        ],
        compiler_params=_params("parallel", "parallel"),
        name="kv_proj",
    )(x, g.reshape(1, d).astype(_F32), wk, wvt)


def _q_proj_kernel(x_ref, g_ref, wqt_ref, qt_ref, *, scale):
    hb = _rms_rows(x_ref[0], g_ref[...]).astype(_BF16)
    qt_ref[0] = (_dot_nt(wqt_ref[...], hb) * scale).astype(_BF16)


def _q_proj(x, g, w_q, scale):
    b, s, d = x.shape
    tm = PROJ_ROWS
    qk = w_q.shape[1]
    wqt = w_q.T.astype(_BF16)
    return pl.pallas_call(
        functools.partial(_q_proj_kernel, scale=scale),
        grid=(b, s // tm),
        in_specs=[
            pl.BlockSpec((1, tm, d), lambda i, j: (i, j, 0)),
            _resident((1, d)),
            _resident(wqt.shape),
        ],
        out_specs=pl.BlockSpec((1, qk, tm), lambda i, j: (i, 0, j)),
        out_shape=jax.ShapeDtypeStruct((b, qk, s), _BF16),
        compiler_params=_params("parallel", "parallel"),
        name="q_proj",
    )(x, g.reshape(1, d).astype(_F32), wqt)


def _diff_attn_kernel(qt_ref, k_ref, vt_ref, lq1_ref, lk1_ref, lq2_ref, lk2_ref, g_ref, o_ref,
                      qcat_ref, s_ref, m_ref, acc_ref, *, hd, lam_init, hps):
    qi = pl.program_id(2)
    tq = qt_ref.shape[2]
    tk = vt_ref.shape[3]
    dv = vt_ref.shape[2] // hps
    dqk = 2 * hd
    ntile = 2 * hps
    row = lax.broadcasted_iota(jnp.int32, (dqk, tq), 0)
    for h in range(hps):
        qt = qt_ref[0, h * dqk:(h + 1) * dqk, :]
        zero = jnp.zeros_like(qt)
        qcat_ref[2 * h] = jnp.where(row < hd, qt, zero)
        qcat_ref[2 * h + 1] = jnp.where(row >= hd, qt, zero)
    m_ref[...] = jnp.full(m_ref.shape, -jnp.inf, _F32)
    acc_ref[...] = jnp.zeros(acc_ref.shape, _F32)
    ones_rows = jnp.ones((ATTN_DEN_ROWS, tk), _BF16)

    def scores(j, t):
        h = t // 2
        k = k_ref[0, pl.ds(pl.multiple_of(j * tk, tk), tk), h * dqk:(h + 1) * dqk]
        return _dot(k, qcat_ref[t])

    def accumulate(j, t, sc, masked):
        h = t // 2
        if masked:
            kpos = lax.broadcasted_iota(jnp.int32, (tk, tq), 0)
            qpos = lax.broadcasted_iota(jnp.int32, (tk, tq), 1)
            sc = jnp.where(qpos >= kpos, sc, -jnp.inf)
        m = m_ref[t]
        m_new = jnp.maximum(m, jnp.max(sc, axis=0, keepdims=True))
        alpha = jnp.exp2(m - m_new)
        m_ref[t] = m_new
        p = jnp.exp2(sc - m_new).astype(_BF16)
        vt = jnp.concatenate([vt_ref[0, j, h * dv:(h + 1) * dv, :], ones_rows], axis=0)
        acc_ref[t] = alpha * acc_ref[t] + _dot(vt, p)

    def block(j, masked):
        sc = s_ref[...]
        for t in range(ntile):
            if t + 1 < ntile:
                nxt = scores(j, t + 1)
            elif not masked:
                s_ref[...] = scores(j + 1, 0)
            accumulate(j, t, sc, masked)
            sc = nxt

    s_ref[...] = scores(0, 0)

    def body(j, carry):
        block(j, False)
        return carry

    lax.fori_loop(0, qi, body, 0)
    block(qi, True)

    lam = (jnp.exp(jnp.sum(lq1_ref[...] * lk1_ref[...], axis=1, keepdims=True))
           - jnp.exp(jnp.sum(lq2_ref[...] * lk2_ref[...], axis=1, keepdims=True)) + lam_init)
    for h in range(hps):
        a1 = acc_ref[2 * h]
        a2 = acc_ref[2 * h + 1]
        o = a1[0:dv] * (1.0 / a1[dv:dv + 1]) - a2[0:dv] * (lam / a2[dv:dv + 1])
        ms = jnp.mean(o * o, axis=0, keepdims=True)
        o = o * (lax.rsqrt(ms + RMS_EPS) * (1.0 - lam_init)) * g_ref[...]
        o_ref[0, :, h * dv:(h + 1) * dv] = o.T.astype(_BF16)


def _diff_attn(qt, k, vtb, lq1, lk1, lq2, lk2, subln_g, n_heads, hd, lam_init):
    b, qk, s = qt.shape
    tq = ATTN_BLOCK
    hps = ATTN_HEADS_PER_STEP
    nkb, nv, tk = vtb.shape[1:]
    dv = nv // n_heads
    assert tq == tk and n_heads % hps == 0
    lam_args = [a.reshape(1, hd).astype(_F32) for a in (lq1, lk1, lq2, lk2)]
    kern = functools.partial(_diff_attn_kernel, hd=hd, lam_init=lam_init, hps=hps)
    return pl.pallas_call(
        kern,
        grid=(b, n_heads // hps, s // tq),
        in_specs=[
            pl.BlockSpec((1, hps * 2 * hd, tq), lambda i, h, j: (i, h, j)),
            pl.BlockSpec((1, s, hps * 2 * hd), lambda i, h, j: (i, 0, h)),
            pl.BlockSpec((1, nkb, hps * dv, tk), lambda i, h, j: (i, 0, h, 0)),
            _resident((1, hd)), _resident((1, hd)), _resident((1, hd)), _resident((1, hd)),
            _resident((dv, 1)),
        ],
        out_specs=pl.BlockSpec((1, tq, hps * dv), lambda i, h, j: (i, j, h)),
        out_shape=jax.ShapeDtypeStruct((b, s, nv), _BF16),
        scratch_shapes=[
            pltpu.VMEM((2 * hps, 2 * hd, tq), _BF16),
            pltpu.VMEM((tk, tq), _F32),
            pltpu.VMEM((2 * hps, 1, tq), _F32),
            pltpu.VMEM((2 * hps, dv + ATTN_DEN_ROWS, tq), _F32),
        ],
        compiler_params=_params("parallel", "parallel", "arbitrary"),
        name="diff_attn",
    )(qt, k, vtb, *lam_args, subln_g.reshape(dv, 1).astype(_F32))


def kernel(x, attn_norm_g, ffn_norm_g, mlstm_w_in, mlstm_b_gate, mlstm_head_g, mlstm_w_out, kv_norm_g, w_kv, diff_w_q, diff_lambda_q1, diff_lambda_k1, diff_lambda_q2, diff_lambda_k2, diff_subln_g, diff_w_out, ffn_w_up, ffn_conv_w, ffn_conv_b, ffn_w_down, final_norm_g):
    depth = ffn_w_up.shape[0]
    n_a = mlstm_w_in.shape[0]
    ha, dv_a = mlstm_head_g.shape[1:]
    dk_a = (mlstm_w_in.shape[2] - 2 * ha * dv_a - 2 * ha) // (2 * ha)
    hd_b = diff_lambda_q1.shape[1]
    qk_b = diff_w_q.shape[2]
    hb = qk_b // (2 * hd_b)

    k_sh = vt_sh = None
    for l in range(depth):
        if l == n_a:
            k_sh, vt_sh = _kv_proj(x, kv_norm_g, w_kv, qk_b)
        if l < n_a:
            q, v, o, kt, gates = _mlstm_proj(x, attn_norm_g[l], mlstm_w_in[l], mlstm_b_gate[l], ha, dk_a, dv_a)
            y = _mlstm(q, kt, v, gates, o, mlstm_head_g[l], ha, dk_a, dv_a)
            w_out = mlstm_w_out[l]
        else:
            j = l - n_a
            lam_init = 0.8 - 0.6 * math.exp(-0.3 * l)
            qt = _q_proj(x, attn_norm_g[l], diff_w_q[j], hd_b ** -0.5 * math.log2(math.e))
            y = _diff_attn(qt, k_sh, vt_sh, diff_lambda_q1[j], diff_lambda_k1[j], diff_lambda_q2[j],
                           diff_lambda_k2[j], diff_subln_g[j], hb, hd_b, lam_init)
            w_out = diff_w_out[j]
        x = _out_ffn(y, x, w_out, ffn_norm_g[l], ffn_w_up[l], ffn_conv_w[l], ffn_conv_b[l], ffn_w_down[l],
                     final_g=final_norm_g if l == depth - 1 else None)
    return x
```
